```python
import jax, jax.numpy as jnp
from jax import lax
import numpy as np

D_MODEL = 4096
BATCH = 1
SEQ = 8192
DEPTH = 1
DEC_BATCH = 128
DEC_SEQ = 4
PAST_LEN = 8192
PAGE_SIZE = 128

N_META = 16
ATTN_WIDTH = D_MODEL // 2
CONV_WIDTH = D_MODEL - ATTN_WIDTH
HEAD_DIM = 64
N_HEADS = ATTN_WIDTH // HEAD_DIM
N_KV_HEADS = N_HEADS // 8
GROUP = N_HEADS // N_KV_HEADS
ROPE_DIM = HEAD_DIM // 4
ROPE_THETA = 500000.0
WINDOW = 128
BLOCK = 128
CONV_K = 3
D_FF = ((8 * D_MODEL // 3 + 255) // 256) * 256
EPS = 1e-5
Q_W = N_HEADS * HEAD_DIM
KV_W = N_KV_HEADS * HEAD_DIM
IN_W = Q_W + 2 * KV_W + 3 * CONV_WIDTH
NEG = -1e30

kernel_name = "hybrid_swa_sink_shortconv_convffn_step"


def rmsnorm(x, g):
    xf = x.astype(jnp.float32)
    y = xf * lax.rsqrt(jnp.mean(xf * xf, axis=-1, keepdims=True) + EPS) * g.astype(jnp.float32)
    return y.astype(x.dtype)


def rope(x, pos):
    half = ROPE_DIM // 2
    inv = ROPE_THETA ** (-jnp.arange(half, dtype=jnp.float32) * 2.0 / ROPE_DIM)
    ang = pos.astype(jnp.float32)[:, None] * inv
    cos = jnp.cos(ang)[:, None, :]
    sin = jnp.sin(ang)[:, None, :]
    x1 = x[..., :half].astype(jnp.float32)
    x2 = x[..., half:ROPE_DIM].astype(jnp.float32)
    r1 = (x1 * cos - x2 * sin).astype(x.dtype)
    r2 = (x2 * cos + x1 * sin).astype(x.dtype)
    return jnp.concatenate([r1, r2, x[..., ROPE_DIM:]], axis=-1)


def sink_attend(q, k, v, mask, sinks):
    s = jnp.einsum('...qkgd,...skd->...kgqs', q.astype(jnp.float32), k.astype(jnp.float32)) * (HEAD_DIM ** -0.5)
    s = jnp.where(mask, s, NEG)
    sink = sinks.astype(jnp.float32).reshape(N_KV_HEADS, GROUP)[:, :, None, None]
    m = jnp.maximum(jnp.max(s, axis=-1, keepdims=True), sink)
    p = jnp.exp(s - m)
    p = p / (jnp.sum(p, axis=-1, keepdims=True) + jnp.exp(sink - m))
    o = jnp.einsum('...kgqs,...skd->...qkgd', p, v.astype(jnp.float32))
    return o.astype(q.dtype)


def attn_prompt(q, k, v, sinks):
    B, L = q.shape[0], q.shape[1]
    pad = (-N_META) % BLOCK
    Lp = L + pad
    nb = Lp // BLOCK
    qb = jnp.pad(q, ((0, 0), (pad, 0), (0, 0), (0, 0))).reshape(B, nb, BLOCK, N_KV_HEADS, GROUP, HEAD_DIM)
    kc = jnp.pad(k, ((0, 0), (pad, 0), (0, 0), (0, 0))).reshape(B, nb, BLOCK, N_KV_HEADS, HEAD_DIM)
    vc = jnp.pad(v, ((0, 0), (pad, 0), (0, 0), (0, 0))).reshape(B, nb, BLOCK, N_KV_HEADS, HEAD_DIM)
    bpad = ((0, 0), (1, 0), (0, 0), (0, 0), (0, 0))
    kb = jnp.concatenate([jnp.pad(kc, bpad)[:, :-1], kc], axis=2)
    vb = jnp.concatenate([jnp.pad(vc, bpad)[:, :-1], vc], axis=2)
    qpos = (jnp.arange(Lp) - pad).reshape(nb, BLOCK)
    kpos = jnp.concatenate([qpos - BLOCK, qpos], axis=1)
    qq = qpos[:, :, None]
    kk = kpos[:, None, :]
    mask = (kk >= 0) & (kk <= qq) & (kk > qq - WINDOW)
    o = sink_attend(qb, kb, vb, mask[None, :, None, None], sinks)
    o = o.reshape(B, Lp, Q_W)[:, pad:]
    return o, k[:, -WINDOW:], v[:, -WINDOW:]


def attn_sample(q, k, v, sinks, cache_k, cache_v):
    DB, T = q.shape[0], q.shape[1]
    W = cache_k.shape[1]
    kk_all = jnp.concatenate([cache_k.astype(k.dtype), k], axis=1)
    vv_all = jnp.concatenate([cache_v.astype(v.dtype), v], axis=1)
    kpos = jnp.arange(W + T) + (PAST_LEN - W)
    qpos = jnp.arange(T) + PAST_LEN
    qq = qpos[:, None]
    kk = kpos[None, :]
    mask = (kk <= qq) & (kk > qq - WINDOW)
    o = sink_attend(q.reshape(DB, T, N_KV_HEADS, GROUP, HEAD_DIM), kk_all, vv_all, mask, sinks)
    return o.reshape(DB, T, Q_W), kk_all[:, -W:], vv_all[:, -W:]


def causal_dwconv(u, w, prev):
    T = u.shape[1]
    up = jnp.concatenate([prev.astype(u.dtype), u], axis=1)
    y = up[:, 0:T] * w[0]
    for i in range(1, CONV_K):
        y = y + up[:, i:i + T] * w[i]
    return y, up[:, -(CONV_K - 1):]


def layer(h, pos, attn_fn, conv_prev, ffn_prev, g_mix, w_in, sinks, conv_w, g_attn_out,
          g_conv_out, w_out, g_ffn, w_gate_up, ffn_conv_w, w_down):
    B, T = h.shape[0], h.shape[1]
    a = rmsnorm(h, g_mix)
    proj = a @ w_in
    splits = [Q_W, Q_W + KV_W, Q_W + 2 * KV_W, Q_W + 2 * KV_W + CONV_WIDTH, Q_W + 2 * KV_W + 2 * CONV_WIDTH]
    q, k, v, cb, cc, cx = jnp.split(proj, splits, axis=-1)
    q = rope(q.reshape(B, T, N_HEADS, HEAD_DIM), pos)
    k = rope(k.reshape(B, T, N_KV_HEADS, HEAD_DIM), pos)
    v = v.reshape(B, T, N_KV_HEADS, HEAD_DIM)
    attn_o, new_k, new_v = attn_fn(q, k, v, sinks)
    conv_y, new_conv = causal_dwconv(cc * cx, conv_w, conv_prev)
    conv_o = cb * conv_y
    mixed = jnp.concatenate([rmsnorm(attn_o, g_attn_out), rmsnorm(conv_o, g_conv_out)], axis=-1) @ w_out
    h = h + mixed
    f = rmsnorm(h, g_ffn)
    gate, up = jnp.split(f @ w_gate_up, [D_FF], axis=-1)
    gate_c, new_ffn = causal_dwconv(gate, ffn_conv_w, ffn_prev)
    h = h + (jax.nn.silu(gate_c) * up) @ w_down
    return h, new_k, new_v, new_conv, new_ffn


def setup_inputs(seed: int = 0) -> dict:
    key = jax.random.key(seed)
    ks = jax.random.split(key, 24)
    f32 = jnp.float32
    nrm = lambda k, shape, s: jax.random.normal(k, shape, f32) * s
    gain = lambda k, shape: 1.0 + 0.02 * jax.random.normal(k, shape, f32)
    return {
        "x_prompt": nrm(ks[0], (BATCH, SEQ, D_MODEL), 1.0),
        "x_sample": nrm(ks[1], (DEC_BATCH, DEC_SEQ, D_MODEL), 1.0),
        "cache_k": nrm(ks[2], (DEPTH, DEC_BATCH, WINDOW, N_KV_HEADS, HEAD_DIM), 1.0),
        "cache_v": nrm(ks[3], (DEPTH, DEC_BATCH, WINDOW, N_KV_HEADS, HEAD_DIM), 1.0),
        "state_conv": nrm(ks[4], (DEPTH, DEC_BATCH, CONV_K - 1, CONV_WIDTH), 1.0),
        "state_ffn_conv": nrm(ks[5], (DEPTH, DEC_BATCH, CONV_K - 1, D_FF), 1.0),
        "meta_tokens": nrm(ks[6], (N_META, D_MODEL), 1.0),
        "g_mix": gain(ks[7], (DEPTH, D_MODEL)),
        "w_in": nrm(ks[8], (DEPTH, D_MODEL, IN_W), D_MODEL ** -0.5),
        "attn_sinks": nrm(ks[9], (DEPTH, N_HEADS), 0.5),
        "conv_w": nrm(ks[10], (DEPTH, CONV_K, CONV_WIDTH), CONV_K ** -0.5),
        "g_attn_out": gain(ks[11], (DEPTH, ATTN_WIDTH)),
        "g_conv_out": gain(ks[12], (DEPTH, CONV_WIDTH)),
        "w_out": nrm(ks[13], (DEPTH, D_MODEL, D_MODEL), D_MODEL ** -0.5),
        "g_ffn": gain(ks[14], (DEPTH, D_MODEL)),
        "w_gate_up": nrm(ks[15], (DEPTH, D_MODEL, 2 * D_FF), D_MODEL ** -0.5),
        "ffn_conv_w": nrm(ks[16], (DEPTH, CONV_K, D_FF), CONV_K ** -0.5),
        "w_down": nrm(ks[17], (DEPTH, D_FF, D_MODEL), D_FF ** -0.5),
        "g_final": gain(ks[18], (D_MODEL,)),
    }


def reference(x_prompt, x_sample, cache_k, cache_v, state_conv, state_ffn_conv, meta_tokens,
              g_mix, w_in, attn_sinks, conv_w, g_attn_out, g_conv_out, w_out, g_ffn,
              w_gate_up, ffn_conv_w, w_down, g_final):
    B = x_prompt.shape[0]
    DB, T = x_sample.shape[0], x_sample.shape[1]
    meta = jnp.broadcast_to(meta_tokens.astype(x_prompt.dtype)[None], (B, N_META, D_MODEL))
    hp = jnp.concatenate([meta, x_prompt], axis=1)
    hs = x_sample
    pos_p = jnp.arange(hp.shape[1])
    pos_s = jnp.arange(T) + PAST_LEN
    nk_p, nv_p, nc_p, nf_p = [], [], [], []
    nk_s, nv_s, nc_s, nf_s = [], [], [], []
    for l in range(DEPTH):
        w = (g_mix[l], w_in[l], attn_sinks[l], conv_w[l], g_attn_out[l], g_conv_out[l], w_out[l],
             g_ffn[l], w_gate_up[l], ffn_conv_w[l], w_down[l])
        zc = jnp.zeros((B, CONV_K - 1, CONV_WIDTH), hp.dtype)
        zf = jnp.zeros((B, CONV_K - 1, D_FF), hp.dtype)
        hp, k1, v1, c1, f1 = layer(hp, pos_p, attn_prompt, zc, zf, *w)
        ck, cv = cache_k[l], cache_v[l]
        samp_fn = lambda q, k, v, s, ck=ck, cv=cv: attn_sample(q, k, v, s, ck, cv)
        hs, k2, v2, c2, f2 = layer(hs, pos_s, samp_fn, state_conv[l], state_ffn_conv[l], *w)
        nk_p.append(k1); nv_p.append(v1); nc_p.append(c1); nf_p.append(f1)
        nk_s.append(k2); nv_s.append(v2); nc_s.append(c2); nf_s.append(f2)
    y_prompt = rmsnorm(hp[:, N_META:], g_final)
    y_sample = rmsnorm(hs, g_final)
    return (y_prompt, y_sample,
            jnp.stack(nk_p), jnp.stack(nv_p), jnp.stack(nc_p), jnp.stack(nf_p),
            jnp.stack(nk_s), jnp.stack(nv_s), jnp.stack(nc_s), jnp.stack(nf_s))
```

```python
import functools

import jax
import jax.numpy as jnp
from jax import lax
from jax.experimental import pallas as pl
from jax.experimental.pallas import tpu as pltpu

N_META = 16
HEAD_DIM = 64
KV_GROUP = 8
ROPE_DIM = HEAD_DIM // 4
ROPE_THETA = 500000.0
WINDOW = 128
BLOCK = 128
CONV_K = 3
EPS = 1e-5
PAST_LEN = 8192
NEG = -1e30

LANES = 128
SUBLANES = 8
BF16_ROWS = 16

ROW_TILE_MAX = 1200
DOWN_ROW_TILE_MAX = 600
DOWN_K_TILE = 1024
SEQ_PAD = 8
SEQS_PER_STEP = 8

f32 = jnp.float32
bf16 = jnp.bfloat16


def _params(semantics, vmem_mib):
    return pltpu.CompilerParams(dimension_semantics=semantics,
                                vmem_limit_bytes=vmem_mib * 2 ** 20)


def _row_tiles(m_rows, max_rows, min_rows, align):
    for nt in range(1, m_rows + 1):
        if m_rows % nt:
            continue
        tm = m_rows // nt
        if tm <= max_rows and tm % align == 0:
            assert tm >= min_rows, (m_rows, tm, min_rows)
            return nt, tm
    raise ValueError(m_rows)


def _lane_sumsq(x):
    acc = None
    for j in range(x.shape[1] // LANES):
        c = x[:, j * LANES:(j + 1) * LANES]
        acc = c * c if acc is None else acc + c * c
    return acc


def _rstd(ssq, width):
    return lax.rsqrt(jnp.sum(ssq, axis=-1, keepdims=True) * (1.0 / width) + EPS)


def _cast_kernel(n_in_blocks, w_ref, o_ref):
    i = pl.program_id(0)

    @pl.when(i < n_in_blocks)
    def _():
        o_ref[...] = w_ref[...].astype(bf16)

    @pl.when(i >= n_in_blocks)
    def _():
        o_ref[...] = jnp.zeros_like(o_ref)


def _cast_bf16(w, br, bc, rows_out=None):
    R, C = w.shape
    rows_out = R if rows_out is None else rows_out
    assert R % br == 0 and rows_out % br == 0 and C % bc == 0
    n_in = R // br
    return pl.pallas_call(
        functools.partial(_cast_kernel, n_in),
        grid=(rows_out // br, C // bc),
        in_specs=[pl.BlockSpec((br, bc), lambda i, j: (jnp.minimum(i, n_in - 1), j))],
        out_specs=pl.BlockSpec((br, bc), lambda i, j: (i, j)),
        out_shape=jax.ShapeDtypeStruct((rows_out, C), bf16),
        compiler_params=_params(("arbitrary", "arbitrary"), 32),
        name="cast_bf16",
    )(w)


def _prenorm_kernel(nbp, head_ref, xp_ref, xs_ref, g_ref, a_ref, h_ref):
    i = pl.program_id(0)

    def emit(x):
        ms = jnp.mean(x * x, axis=-1, keepdims=True)
        a_ref[...] = (x * lax.rsqrt(ms + EPS) * g_ref[...]).astype(bf16)
        h_ref[...] = x

    @pl.when(i == 0)
    def _():
        emit(head_ref[...])

    @pl.when((i >= 1) & (i <= nbp))
    def _():
        emit(xp_ref[...])

    @pl.when(i > nbp)
    def _():
        emit(xs_ref[...])


def _prenorm(head, xp, xs, g):
    D = xp.shape[1]
    nbp = xp.shape[0] // BLOCK
    nbs = xs.shape[0] // BLOCK
    M = BLOCK * (1 + nbp + nbs)
    blk = (BLOCK, D)
    return pl.pallas_call(
        functools.partial(_prenorm_kernel, nbp),
        grid=(1 + nbp + nbs,),
        in_specs=[
            pl.BlockSpec(blk, lambda i: (0, 0)),
            pl.BlockSpec(blk, lambda i: (jnp.clip(i - 1, 0, nbp - 1), 0)),
            pl.BlockSpec(blk, lambda i: (jnp.clip(i - 1 - nbp, 0, nbs - 1), 0)),
            pl.BlockSpec((1, D), lambda i: (0, 0)),
        ],
        out_specs=[pl.BlockSpec(blk, lambda i: (i, 0)), pl.BlockSpec(blk, lambda i: (i, 0))],
        out_shape=[jax.ShapeDtypeStruct((M, D), bf16), jax.ShapeDtypeStruct((M, D), f32)],
        compiler_params=_params(("arbitrary",), 32),
        name="prenorm",
    )(head, xp, xs, g)


def _qkv_kernel(a_ref, w_ref, o_ref):
    o_ref[...] = jnp.dot(a_ref[...], w_ref[...], preferred_element_type=f32)


def _qkv_proj(a, w_in_bf, qkv_w, nt, tm, tn):
    M, D = a.shape
    return pl.pallas_call(
        _qkv_kernel,
        grid=(nt, qkv_w // tn),
        in_specs=[pl.BlockSpec((tm, D), lambda m, n: (m, 0)),
                  pl.BlockSpec((D, tn), lambda m, n: (0, n))],
        out_specs=pl.BlockSpec((tm, tn), lambda m, n: (m, n)),
        out_shape=jax.ShapeDtypeStruct((M, qkv_w), f32),
        compiler_params=_params(("arbitrary", "arbitrary"), 48),
        name="qkv_proj",
    )(a, w_in_bf)


def _shift_conv(buf_ref, carry_ref, idx, x, w_ref, tm, first_tile):
    @pl.when(first_tile)
    def _():
        carry_ref[idx] = jnp.zeros(carry_ref.shape[1:], f32)

    buf_ref[0:SUBLANES, :] = carry_ref[idx]
    buf_ref[SUBLANES:SUBLANES + tm, :] = x
    carry_ref[idx] = x[tm - SUBLANES:tm, :]
    return (w_ref[0:1, :] * buf_ref[SUBLANES - 2:SUBLANES - 2 + tm, :]
            + w_ref[1:2, :] * buf_ref[SUBLANES - 1:SUBLANES - 1 + tm, :]
            + w_ref[2:3, :] * x)


def _sample_conv(x, st_ref, w_ref, soff, db, n_t):
    steps = [st_ref[0], st_ref[1]] + [x[soff + t * db:soff + (t + 1) * db, :] for t in range(n_t)]
    for t in range(n_t):
        yield t, (w_ref[0:1, :] * steps[t] + w_ref[1:2, :] * steps[t + 1] + w_ref[2:3, :] * steps[t + 2])


def _conv_branch_kernel(cfg, a_ref, wb_ref, wc_ref, wx_ref, cw_ref, g_ref, st_ref,
                        c_ref, ssq_ref, ukeep_ref, carry_ref, ubuf_ref, co_ref):
    nt, tm, soff, db, n_t = cfg
    m = pl.program_id(0)
    c = pl.program_id(1)
    a = a_ref[...]
    cb = jnp.dot(a, wb_ref[...], preferred_element_type=f32)
    u = (jnp.dot(a, wc_ref[...], preferred_element_type=f32)
         * jnp.dot(a, wx_ref[...], preferred_element_type=f32))
    co_ref[...] = cb * _shift_conv(ubuf_ref, carry_ref, c, u, cw_ref, tm, m == 0)

    @pl.when(m == nt - 1)
    def _():
        ukeep_ref[...] = u[soff - SUBLANES:soff + n_t * db, :]
        for t, y in _sample_conv(u, st_ref, cw_ref, soff, db, n_t):
            rows = slice(soff + t * db, soff + (t + 1) * db)
            co_ref[rows, :] = cb[rows, :] * y

    co = co_ref[...]
    c_ref[...] = (co * g_ref[...]).astype(bf16)
    part = _lane_sumsq(co)

    @pl.when(c == 0)
    def _():
        ssq_ref[...] = part

    @pl.when(c > 0)
    def _():
        ssq_ref[...] += part


def _conv_branch(a, w_in_bf, conv_w, g_conv, st, qkv_w, cw, nt, tm, tc, soff, db, n_t):
    M, D = a.shape
    nct = cw // tc
    keep = SUBLANES + n_t * db
    ob, oc, ox = qkv_w // tc, (qkv_w + cw) // tc, (qkv_w + 2 * cw) // tc
    return pl.pallas_call(
        functools.partial(_conv_branch_kernel, (nt, tm, soff, db, n_t)),
        grid=(nt, nct),
        in_specs=[
            pl.BlockSpec((tm, D), lambda m, c: (m, 0)),
            pl.BlockSpec((D, tc), lambda m, c: (0, ob + c)),
            pl.BlockSpec((D, tc), lambda m, c: (0, oc + c)),
            pl.BlockSpec((D, tc), lambda m, c: (0, ox + c)),
            pl.BlockSpec((CONV_K, tc), lambda m, c: (0, c)),
            pl.BlockSpec((1, tc), lambda m, c: (0, c)),
            pl.BlockSpec((CONV_K - 1, db, tc), lambda m, c: (0, 0, c)),
        ],
        out_specs=[
            pl.BlockSpec((tm, tc), lambda m, c: (m, c)),
            pl.BlockSpec((tm, LANES), lambda m, c: (m, 0)),
            pl.BlockSpec((keep, tc), lambda m, c: (0, jnp.where(m == nt - 1, c, 0))),
        ],
        out_shape=[
            jax.ShapeDtypeStruct((M, cw), bf16),
            jax.ShapeDtypeStruct((M, LANES), f32),
            jax.ShapeDtypeStruct((keep, cw), f32),
        ],
        scratch_shapes=[
            pltpu.VMEM((nct, SUBLANES, tc), f32),
            pltpu.VMEM((tm + SUBLANES, tc), f32),
            pltpu.VMEM((tm, tc), f32),
        ],
        compiler_params=_params(("arbitrary", "arbitrary"), 48),
        name="conv_branch",
    )(a, w_in_bf, w_in_bf, w_in_bf, conv_w, g_conv, st)


def _rope_tables(pos):
    half = ROPE_DIM // 2
    inv = ROPE_THETA ** (-jnp.arange(half, dtype=f32) * 2.0 / ROPE_DIM)
    ang = pos.astype(f32)[:, None] * inv
    cos, sin = jnp.cos(ang), jnp.sin(ang)
    rows = pos.shape[0]
    one = jnp.ones((rows, HEAD_DIM - ROPE_DIM), f32)
    zero = jnp.zeros((rows, HEAD_DIM - ROPE_DIM), f32)
    zh = jnp.zeros((rows, half), f32)
    c64 = jnp.concatenate([cos, cos, one], axis=-1)
    s1 = jnp.concatenate([zh, sin, zero], axis=-1)
    s2 = jnp.concatenate([-sin, zh, zero], axis=-1)
    two = lambda t: jnp.concatenate([t, t], axis=-1)
    return two(c64), two(s1), two(s2)


def _rope(x, cos, s1, s2):
    return x * cos + pltpu.roll(x, ROPE_DIM // 2, 1) * s1 + pltpu.roll(x, LANES - ROPE_DIM // 2, 1) * s2


def _both_halves(x, low):
    r = pltpu.roll(x, HEAD_DIM, 1)
    return jnp.where(low, x, r), jnp.where(low, r, x)


def _sink_softmax(s, sink):
    m = jnp.maximum(jnp.max(s, axis=-1, keepdims=True), sink)
    p = jnp.exp(s - m)
    return p, jnp.sum(p, axis=-1, keepdims=True) + jnp.exp(sink - m)


_NT_DIMS = (((1,), (1,)), ((), ()))


def _attn_prompt_kernel(cfg, sink_ref, q_ref, kv_ref, cos_ref, s1_ref, s2_ref, g_ref, as_ref, ssqs_ref,
                        a_ref, ssq_ref, kout_ref, vout_ref, kkp_ref, vvp_ref):
    nbp1, q_w, kv_w, pad = cfg
    b = pl.program_id(0)

    @pl.when(b < nbp1)
    def _():
        cos, s1, s2 = cos_ref[...], s1_ref[...], s2_ref[...]
        low = lax.broadcasted_iota(jnp.int32, (BLOCK, LANES), 1) < HEAD_DIM

        @pl.when(b == 0)
        def _():
            kkp_ref[...] = jnp.zeros_like(kkp_ref)
            vvp_ref[...] = jnp.zeros_like(vvp_ref)

        kv = kv_ref[...]
        kk_cur, vv_cur = [], []
        for ch in range(kv_w // LANES):
            kc = _rope(kv[:, ch * LANES:(ch + 1) * LANES], cos, s1, s2)
            vc = kv[:, kv_w + ch * LANES:kv_w + (ch + 1) * LANES]
            kout_ref[:, ch * LANES:(ch + 1) * LANES] = kc
            kk_cur += [t.astype(bf16) for t in _both_halves(kc, low)]
            vv_cur += [t.astype(bf16) for t in _both_halves(vc, low)]
        vout_ref[...] = kv[:, kv_w:]
        k2 = [jnp.concatenate([kkp_ref[g], kk_cur[g]], axis=0) for g in range(len(kk_cur))]
        v2 = [jnp.concatenate([vvp_ref[g], vv_cur[g]], axis=0) for g in range(len(vv_cur))]

        r = lax.broadcasted_iota(jnp.int32, (BLOCK, 2 * BLOCK), 0)
        j = lax.broadcasted_iota(jnp.int32, (BLOCK, 2 * BLOCK), 1)
        jmin = jnp.maximum(0, BLOCK + pad - BLOCK * b)
        mask = (j > r) & (j <= r + BLOCK) & (j >= jmin)

        ssq = jnp.zeros((BLOCK, LANES), f32)
        for c in range(q_w // LANES):
            g = (2 * c) // KV_GROUP
            cols = slice(c * LANES, (c + 1) * LANES)
            qc = _rope(q_ref[:, cols], cos, s1, s2) * (HEAD_DIM ** -0.5)
            q2 = jnp.concatenate([jnp.where(low, qc, 0.0).astype(bf16),
                                  jnp.where(low, 0.0, qc).astype(bf16)], axis=0)
            s = lax.dot_general(q2, k2[g], _NT_DIMS, preferred_element_type=f32)
            ps, dens = [], []
            for h in range(2):
                p, den = _sink_softmax(jnp.where(mask, s[h * BLOCK:(h + 1) * BLOCK], NEG),
                                       sink_ref[0, 2 * c + h])
                ps.append(p.astype(bf16))
                dens.append(den)
            o = jnp.dot(jnp.concatenate(ps, axis=0), v2[g], preferred_element_type=f32)
            oc = jnp.where(low, o[:BLOCK] / dens[0], o[BLOCK:] / dens[1])
            a_ref[:, cols] = (oc * g_ref[:, cols]).astype(bf16)
            ssq = ssq + oc * oc
        ssq_ref[...] = ssq
        for g in range(len(kk_cur)):
            kkp_ref[g] = kk_cur[g]
            vvp_ref[g] = vv_cur[g]

    @pl.when(b >= nbp1)
    def _():
        a_ref[...] = as_ref[...]
        ssq_ref[...] = ssqs_ref[...]


def _attn_prompt(qkv, sinks, tabs, g_attn, a_s, ssq_s, q_w, kv_w, nbp1, nbs, pad):
    M = qkv.shape[0]
    n_kv = kv_w // HEAD_DIM
    kvb = q_w // (2 * kv_w)
    pb = lambda b: jnp.minimum(b, nbp1 - 1)
    sb = lambda b: jnp.clip(b - nbp1, 0, nbs - 1)
    tab_spec = pl.BlockSpec((BLOCK, LANES), lambda b: (pb(b), 0))
    return pl.pallas_call(
        functools.partial(_attn_prompt_kernel, (nbp1, q_w, kv_w, pad)),
        grid=(nbp1 + nbs,),
        in_specs=[
            pl.BlockSpec(memory_space=pltpu.SMEM),
            pl.BlockSpec((BLOCK, q_w), lambda b: (pb(b), 0)),
            pl.BlockSpec((BLOCK, 2 * kv_w), lambda b: (pb(b), kvb)),
            tab_spec, tab_spec, tab_spec,
            pl.BlockSpec((1, q_w), lambda b: (0, 0)),
            pl.BlockSpec((BLOCK, q_w), lambda b: (sb(b), 0)),
            pl.BlockSpec((BLOCK, LANES), lambda b: (sb(b), 0)),
        ],
        out_specs=[
            pl.BlockSpec((BLOCK, q_w), lambda b: (b, 0)),
            pl.BlockSpec((BLOCK, LANES), lambda b: (b, 0)),
            pl.BlockSpec((BLOCK, kv_w), lambda b: (0, 0)),
            pl.BlockSpec((BLOCK, kv_w), lambda b: (0, 0)),
        ],
        out_shape=[
            jax.ShapeDtypeStruct((M, q_w), bf16),
            jax.ShapeDtypeStruct((M, LANES), f32),
            jax.ShapeDtypeStruct((BLOCK, kv_w), f32),
            jax.ShapeDtypeStruct((BLOCK, kv_w), f32),
        ],
        scratch_shapes=[pltpu.VMEM((n_kv, BLOCK, LANES), bf16),
                        pltpu.VMEM((n_kv, BLOCK, LANES), bf16)],
        compiler_params=_params(("arbitrary",), 32),
        name="attn_prompt",
    )(sinks, qkv, qkv, *tabs, g_attn, a_s, ssq_s)


def _attn_sample_kernel(cfg, sink_ref, qkv_ref, ck_ref, cv_ref, cos_ref, s1_ref, s2_ref, g_ref,
                        a_ref, ssq_ref, knew_ref):
    q_w, kv_w, n_t = cfg
    cos, s1, s2 = cos_ref[...], s1_ref[...], s2_ref[...]
    rows = SEQS_PER_STEP * SEQ_PAD
    low_all = lax.broadcasted_iota(jnp.int32, (rows, LANES), 1) < HEAD_DIM
    low_w = lax.broadcasted_iota(jnp.int32, (WINDOW, LANES), 1) < HEAD_DIM
    low8 = low_all[:SEQ_PAD]
    n_qc = q_w // LANES
    n_kc = kv_w // LANES
    cpg = KV_GROUP // 2
    grows = 2 * cpg * SEQ_PAD

    qs = []
    for c in range(n_qc):
        qc = _rope(qkv_ref[:, c * LANES:(c + 1) * LANES], cos, s1, s2) * (HEAD_DIM ** -0.5)
        qs.append((jnp.where(low_all, qc, 0.0).astype(bf16), jnp.where(low_all, 0.0, qc).astype(bf16)))
    kn, vn = [], []
    for ch in range(n_kc):
        kc = _rope(qkv_ref[:, q_w + ch * LANES:q_w + (ch + 1) * LANES], cos, s1, s2)
        knew_ref[:, ch * LANES:(ch + 1) * LANES] = kc
        kn += list(_both_halves(kc, low_all))
        vn += list(_both_halves(qkv_ref[:, q_w + kv_w + ch * LANES:q_w + kv_w + (ch + 1) * LANES], low_all))

    t = lax.broadcasted_iota(jnp.int32, (grows, 2 * WINDOW), 0) & (SEQ_PAD - 1)
    j = lax.broadcasted_iota(jnp.int32, (grows, 2 * WINDOW), 1)
    mask = ((j < WINDOW) & (j > t)) | ((j >= WINDOW) & (j - WINDOW <= t) & (j - WINDOW < n_t))
    zpad = jnp.zeros((WINDOW - SEQ_PAD, LANES), f32)

    outs = [[None] * SEQS_PER_STEP for _ in range(n_qc)]
    for s in range(SEQS_PER_STEP):
        srows = slice(s * SEQ_PAD, (s + 1) * SEQ_PAD)
        kc2, vc2 = [], []
        for ch in range(n_kc):
            kc2 += list(_both_halves(ck_ref[s, :, ch * LANES:(ch + 1) * LANES], low_w))
            vc2 += list(_both_halves(cv_ref[s, :, ch * LANES:(ch + 1) * LANES], low_w))
        for g in range(kv_w // HEAD_DIM):
            k2 = jnp.concatenate([kc2[g], kn[g][srows], zpad], axis=0).astype(bf16)
            v2 = jnp.concatenate([vc2[g], vn[g][srows], zpad], axis=0).astype(bf16)
            q2 = jnp.concatenate([qs[cpg * g + cl][h][srows] for cl in range(cpg) for h in range(2)], axis=0)
            sc = lax.dot_general(q2, k2, _NT_DIMS, preferred_element_type=f32)
            p, den = _sink_softmax(jnp.where(mask, sc, NEG), sink_ref[g * grows:(g + 1) * grows, :])
            o = jnp.dot(p.astype(bf16), v2, preferred_element_type=f32) / den
            for cl in range(cpg):
                base = 2 * cl * SEQ_PAD
                outs[cpg * g + cl][s] = jnp.where(low8, o[base:base + SEQ_PAD],
                                                  o[base + SEQ_PAD:base + 2 * SEQ_PAD])
    ssq = jnp.zeros((rows, LANES), f32)
    for c in range(n_qc):
        oc = jnp.concatenate(outs[c], axis=0)
        cols = slice(c * LANES, (c + 1) * LANES)
        a_ref[:, cols] = (oc * g_ref[:, cols]).astype(bf16)
        ssq = ssq + oc * oc
    ssq_ref[...] = ssq


def _attn_sample(qkv8, sink_rows, ck, cv, tabs, g_attn, q_w, kv_w, n_t):
    rows_all, qkv_w = qkv8.shape
    rows = SEQS_PER_STEP * SEQ_PAD
    db = rows_all // SEQ_PAD
    tab_spec = pl.BlockSpec((rows, LANES), lambda i: (0, 0))
    return pl.pallas_call(
        functools.partial(_attn_sample_kernel, (q_w, kv_w, n_t)),
        grid=(db // SEQS_PER_STEP,),
        in_specs=[
            pl.BlockSpec(sink_rows.shape, lambda i: (0, 0)),
            pl.BlockSpec((rows, qkv_w), lambda i: (i, 0)),
            pl.BlockSpec((SEQS_PER_STEP, WINDOW, kv_w), lambda i: (i, 0, 0)),
            pl.BlockSpec((SEQS_PER_STEP, WINDOW, kv_w), lambda i: (i, 0, 0)),
            tab_spec, tab_spec, tab_spec,
            pl.BlockSpec((1, q_w), lambda i: (0, 0)),
        ],
        out_specs=[
            pl.BlockSpec((rows, q_w), lambda i: (i, 0)),
            pl.BlockSpec((rows, LANES), lambda i: (i, 0)),
            pl.BlockSpec((rows, kv_w), lambda i: (i, 0)),
        ],
        out_shape=[
            jax.ShapeDtypeStruct((rows_all, q_w), bf16),
            jax.ShapeDtypeStruct((rows_all, LANES), f32),
            jax.ShapeDtypeStruct((rows_all, kv_w), f32),
        ],
        compiler_params=_params(("arbitrary",), 32),
        name="attn_sample",
    )(sink_rows, qkv8, ck, cv, *tabs, g_attn)


def _out_proj_kernel(cfg, a_ref, c_ref, ssa_ref, ssc_ref, wa_ref, wc_ref, h0_ref, g_ref,
                     h1_ref, f_ref, ssf_ref):
    q_w, cw = cfg
    n = pl.program_id(1)
    mixed = (_rstd(ssa_ref[...], q_w) * jnp.dot(a_ref[...], wa_ref[...], preferred_element_type=f32)
             + _rstd(ssc_ref[...], cw) * jnp.dot(c_ref[...], wc_ref[...], preferred_element_type=f32))
    h1 = h0_ref[...] + mixed
    h1_ref[...] = h1
    f_ref[...] = (h1 * g_ref[...]).astype(bf16)
    part = _lane_sumsq(h1)

    @pl.when(n == 0)
    def _():
        ssf_ref[...] = part

    @pl.when(n > 0)
    def _():
        ssf_ref[...] += part


def _out_proj(a, c, ssa, ssc, w_out_bf, h0, g_ffn, nt, tm, tn):
    M, q_w = a.shape
    cw = c.shape[1]
    D = h0.shape[1]
    assert q_w == cw
    return pl.pallas_call(
        functools.partial(_out_proj_kernel, (q_w, cw)),
        grid=(nt, D // tn),
        in_specs=[
            pl.BlockSpec((tm, q_w), lambda m, n: (m, 0)),
            pl.BlockSpec((tm, cw), lambda m, n: (m, 0)),
            pl.BlockSpec((tm, LANES), lambda m, n: (m, 0)),
            pl.BlockSpec((tm, LANES), lambda m, n: (m, 0)),
            pl.BlockSpec((q_w, tn), lambda m, n: (0, n)),
            pl.BlockSpec((cw, tn), lambda m, n: (1, n)),
            pl.BlockSpec((tm, tn), lambda m, n: (m, n)),
            pl.BlockSpec((1, tn), lambda m, n: (0, n)),
        ],
        out_specs=[
            pl.BlockSpec((tm, tn), lambda m, n: (m, n)),
            pl.BlockSpec((tm, tn), lambda m, n: (m, n)),
            pl.BlockSpec((tm, LANES), lambda m, n: (m, 0)),
        ],
        out_shape=[
            jax.ShapeDtypeStruct((M, D), f32),
            jax.ShapeDtypeStruct((M, D), bf16),
            jax.ShapeDtypeStruct((M, LANES), f32),
        ],
        compiler_params=_params(("arbitrary", "arbitrary"), 48),
        name="out_proj",
    )(a, c, ssa, ssc, w_out_bf, w_out_bf, h0, g_ffn)


def _ffn_up_kernel(cfg, f_ref, ssf_ref, wg_ref, wu_ref, cw_ref, st_ref,
                   act_ref, gkeep_ref, carry_ref, gbuf_ref):
    nt, tm, soff, db, n_t, nft, d_model = cfg
    m = pl.program_id(0)
    n = pl.program_id(1)

    @pl.when(n < nft)
    def _():
        rf = _rstd(ssf_ref[...], d_model)
        f = f_ref[...]
        g = rf * jnp.dot(f, wg_ref[...], preferred_element_type=f32)
        u = rf * jnp.dot(f, wu_ref[...], preferred_element_type=f32)
        gc = _shift_conv(gbuf_ref, carry_ref, n, g, cw_ref, tm, m == 0)
        act_ref[...] = (jax.nn.silu(gc) * u).astype(bf16)

        @pl.when(m == nt - 1)
        def _():
            gkeep_ref[...] = g[soff - SUBLANES:soff + n_t * db, :]
            for t, gct in _sample_conv(g, st_ref, cw_ref, soff, db, n_t):
                rows = slice(soff + t * db, soff + (t + 1) * db)
                act_ref[rows, :] = (jax.nn.silu(gct) * u[rows, :]).astype(bf16)

    @pl.when(n >= nft)
    def _():
        act_ref[...] = jnp.zeros_like(act_ref)


def _ffn_up(fx, ssf, w_gu_bf, ffn_conv_w, st, d_ff, dfp, nt, tm, tn, soff, db, n_t):
    M, D = fx.shape
    nft = d_ff // tn
    keep = SUBLANES + n_t * db
    cl = lambda n: jnp.minimum(n, nft - 1)
    return pl.pallas_call(
        functools.partial(_ffn_up_kernel, (nt, tm, soff, db, n_t, nft, D)),
        grid=(nt, dfp // tn),
        in_specs=[
            pl.BlockSpec((tm, D), lambda m, n: (m, 0)),
            pl.BlockSpec((tm, LANES), lambda m, n: (m, 0)),
            pl.BlockSpec((D, tn), lambda m, n: (0, cl(n))),
            pl.BlockSpec((D, tn), lambda m, n: (0, nft + cl(n))),
            pl.BlockSpec((CONV_K, tn), lambda m, n: (0, cl(n))),
            pl.BlockSpec((CONV_K - 1, db, tn), lambda m, n: (0, 0, cl(n))),
        ],
        out_specs=[
            pl.BlockSpec((tm, tn), lambda m, n: (m, n)),
            pl.BlockSpec((keep, tn), lambda m, n: (0, jnp.where(m == nt - 1, cl(n), 0))),
        ],
        out_shape=[
            jax.ShapeDtypeStruct((M, dfp), bf16),
            jax.ShapeDtypeStruct((keep, d_ff), f32),
        ],
        scratch_shapes=[
            pltpu.VMEM((nft, SUBLANES, tn), f32),
            pltpu.VMEM((tm + SUBLANES, tn), f32),
        ],
        compiler_params=_params(("arbitrary", "arbitrary"), 48),
        name="ffn_up",
    )(fx, ssf, w_gu_bf, w_gu_bf, ffn_conv_w, st)


def _ffn_down_kernel(cfg, act_ref, w_ref, h1_ref, g_ref, y_ref):
    nk, n_res, d_model = cfg
    k = pl.program_id(1)
    act = act_ref[...]
    for jc in range(n_res):
        cols = slice(jc * DOWN_K_TILE, (jc + 1) * DOWN_K_TILE)
        d = jnp.dot(act, w_ref[:, cols], preferred_element_type=f32)

        @pl.when(k == 0)
        def _():
            y_ref[:, cols] = d

        @pl.when(k > 0)
        def _():
            y_ref[:, cols] += d

        @pl.when(k == jc)
        def _():
            y_ref[:, cols] += h1_ref[...]

    @pl.when(k == nk - 1)
    def _():
        y = y_ref[...]
        ms = jnp.mean(y * y, axis=-1, keepdims=True)
        y_ref[...] = y * lax.rsqrt(ms + EPS) * g_ref[...]


def _ffn_down(act, w_down_bf, h1, g_final, nt5, tm5):
    M, dfp = act.shape
    D = h1.shape[1]
    nk = dfp // DOWN_K_TILE
    n_res = D // DOWN_K_TILE
    assert D % DOWN_K_TILE == 0 and nk >= n_res
    return pl.pallas_call(
        functools.partial(_ffn_down_kernel, (nk, n_res, D)),
        grid=(nt5, nk),
        in_specs=[
            pl.BlockSpec((tm5, DOWN_K_TILE), lambda m, k: (m, k)),
            pl.BlockSpec((DOWN_K_TILE, D), lambda m, k: (k, 0)),
            pl.BlockSpec((tm5, DOWN_K_TILE), lambda m, k: (m, jnp.minimum(k, n_res - 1))),
            pl.BlockSpec((1, D), lambda m, k: (0, 0)),
        ],
        out_specs=pl.BlockSpec((tm5, D), lambda m, k: (m, 0)),
        out_shape=jax.ShapeDtypeStruct((M, D), f32),
        compiler_params=_params(("arbitrary", "arbitrary"), 56),
        name="ffn_down",
    )(act, w_down_bf, h1, g_final)


def _time_major(x):
    return jnp.swapaxes(x, 0, 1).reshape((-1,) + x.shape[2:])


def kernel(x_prompt, x_sample, cache_k, cache_v, state_conv, state_ffn_conv, meta_tokens,
           g_mix, w_in, attn_sinks, conv_w, g_attn_out, g_conv_out, w_out, g_ffn,
           w_gate_up, ffn_conv_w, w_down, g_final):
    B, SEQ, D = x_prompt.shape
    DB, T, _ = x_sample.shape
    depth = g_mix.shape[0]
    q_w = g_attn_out.shape[1]
    cw = g_conv_out.shape[1]
    d_ff = ffn_conv_w.shape[2]
    n_kv = cache_k.shape[3]
    kv_w = n_kv * HEAD_DIM
    qkv_w = q_w + 2 * kv_w
    n_heads = attn_sinks.shape[1]
    assert B == 1 and depth == 1 and WINDOW == BLOCK and cache_k.shape[2] == WINDOW
    assert meta_tokens.shape[0] == N_META and SEQ % BLOCK == 0 and (DB * T) % BLOCK == 0
    assert n_heads * HEAD_DIM == q_w and n_heads == n_kv * KV_GROUP
    assert kv_w % LANES == 0 and q_w % (2 * kv_w) == 0 and T <= SEQ_PAD and DB % SEQS_PER_STEP == 0
    assert state_conv.shape[2] == CONV_K - 1 and conv_w.shape[1] == CONV_K

    pad = (-N_META) % BLOCK
    rows_p = pad + N_META + SEQ
    s_rows = DB * T
    M = rows_p + s_rows
    nbp1 = rows_p // BLOCK
    nbs = s_rows // BLOCK
    nt, tm = _row_tiles(M, ROW_TILE_MAX, s_rows + SUBLANES, BF16_ROWS)
    soff = rows_p - (nt - 1) * tm
    assert soff >= SUBLANES and soff % BF16_ROWS == 0 and DB % BF16_ROWS == 0
    nt5, tm5 = _row_tiles(M, DOWN_ROW_TILE_MAX, 1, SUBLANES)
    tn_qkv = 512 if qkv_w % 512 == 0 else LANES
    tc = 256 if (qkv_w % 256 == 0 and cw % 256 == 0) else LANES
    tn_out = 512 if D % 512 == 0 else LANES
    tn_ff = 256
    assert d_ff % tn_ff == 0
    dfp = -(-d_ff // DOWN_K_TILE) * DOWN_K_TILE

    wbr = 2048 if D % 2048 == 0 else D
    w_in_bf = _cast_bf16(w_in[0], wbr, 512 if w_in.shape[2] % 512 == 0 else LANES)
    w_out_bf = _cast_bf16(w_out[0], wbr, 512 if D % 512 == 0 else LANES)
    w_gu_bf = _cast_bf16(w_gate_up[0], wbr, 512 if (2 * d_ff) % 512 == 0 else LANES)
    w_down_bf = _cast_bf16(w_down[0], 256, D, rows_out=dfp)

    head = jnp.concatenate([jnp.zeros((pad, D), f32), meta_tokens.astype(f32)], axis=0)
    a, h0 = _prenorm(head, x_prompt[0], _time_major(x_sample), g_mix)

    qkv = _qkv_proj(a, w_in_bf, qkv_w, nt, tm, tn_qkv)
    st_conv = jnp.swapaxes(state_conv[0], 0, 1)
    c_act, ssq_c, u_keep = _conv_branch(a, w_in_bf, conv_w[0], g_conv_out, st_conv,
                                        qkv_w, cw, nt, tm, tc, soff, DB, T)

    qkv_s = qkv[rows_p:].reshape(T, DB, qkv_w)
    qkv8 = jnp.pad(jnp.swapaxes(qkv_s, 0, 1), ((0, 0), (0, SEQ_PAD - T), (0, 0))).reshape(DB * SEQ_PAD, qkv_w)
    pos_s = PAST_LEN + jnp.arange(SEQ_PAD)
    tabs_s = [jnp.tile(t, (SEQS_PER_STEP, 1)) for t in _rope_tables(pos_s)]
    sink_rows = jnp.repeat(attn_sinks[0].astype(f32), SEQ_PAD)[:, None]
    ck = cache_k[0].reshape(DB, WINDOW, kv_w)
    cv = cache_v[0].reshape(DB, WINDOW, kv_w)
    a8, ssq8, knew8 = _attn_sample(qkv8, sink_rows, ck, cv, tabs_s, g_attn_out, q_w, kv_w, T)
    unpad = lambda x: _time_major(x.reshape(DB, SEQ_PAD, x.shape[1])[:, :T])
    a_s, ssq_s = unpad(a8), unpad(ssq8)

    tabs_p = _rope_tables(jnp.arange(rows_p) - pad)
    a_act, ssq_a, k_last, v_last = _attn_prompt(qkv, attn_sinks.astype(f32), tabs_p, g_attn_out,
                                                a_s, ssq_s, q_w, kv_w, nbp1, nbs, pad)

    h1, fx, ssq_f = _out_proj(a_act, c_act, ssq_a, ssq_c, w_out_bf, h0, g_ffn, nt, tm, tn_out)
    st_ffn = jnp.swapaxes(state_ffn_conv[0], 0, 1)
    act, g_keep = _ffn_up(fx, ssq_f, w_gu_bf, ffn_conv_w[0], st_ffn, d_ff, dfp, nt, tm, tn_ff, soff, DB, T)
    y = _ffn_down(act, w_down_bf, h1, g_final[None, :], nt5, tm5)

    y_prompt = y[pad + N_META:rows_p][None]
    y_sample = jnp.swapaxes(y[rows_p:].reshape(T, DB, D), 0, 1)
    new_k_prompt = k_last.reshape(1, 1, WINDOW, n_kv, HEAD_DIM)
    new_v_prompt = v_last.reshape(1, 1, WINDOW, n_kv, HEAD_DIM)

    def last_states(keep):
        prompt = keep[SUBLANES - (CONV_K - 1):SUBLANES][None, None]
        steps = keep[SUBLANES:].reshape(T, DB, keep.shape[1])[T - (CONV_K - 1):]
        return prompt, jnp.swapaxes(steps, 0, 1)[None]

    new_conv_prompt, new_conv_sample = last_states(u_keep)
    new_ffn_prompt, new_ffn_sample = last_states(g_keep)
    k_new = knew8.reshape(DB, SEQ_PAD, n_kv, HEAD_DIM)[:, :T]
    v_new = jnp.swapaxes(qkv_s[:, :, q_w + kv_w:], 0, 1).reshape(DB, T, n_kv, HEAD_DIM)
    new_k_sample = jnp.concatenate([cache_k[0].astype(f32), k_new], axis=1)[:, -WINDOW:][None]
    new_v_sample = jnp.concatenate([cache_v[0].astype(f32), v_new], axis=1)[:, -WINDOW:][None]
    return (y_prompt, y_sample, new_k_prompt, new_v_prompt, new_conv_prompt, new_ffn_prompt,
            new_k_sample, new_v_sample, new_conv_sample, new_ffn_sample)
```
